```python
import math
import jax, jax.numpy as jnp
from jax import lax
import numpy as np

D_MODEL = 2048
BATCH = 16
SEQ = 2048
DEPTH = 2

HEAD_DIM = 64
N_MIXERS = 4
GROUP_WIDTH = D_MODEL // N_MIXERS
MIX_WIDTH = N_MIXERS * GROUP_WIDTH
SGU_GROUPS = GROUP_WIDTH // HEAD_DIM
SGU_CHUNK = 128
DIL_HEADS = GROUP_WIDTH // HEAD_DIM
DIL_PATTERNS = ((128, 1), (512, 4), (2048, 16))
CONV_CH = GROUP_WIDTH
CONV_WIDTH = 31
GQA_Q_HEADS = GROUP_WIDTH // HEAD_DIM
GQA_KV_HEADS = GQA_Q_HEADS // 4
KV_WIDTH = GQA_KV_HEADS * HEAD_DIM
Q_BLOCK = 128
GRID_W = 64
ROPE_THETA = 10000.0
REL_BUCKETS = 32
REL_MAX_DIST = 1024
FFN_HIDDEN = ((8 * D_MODEL + 3 * 256 - 1) // (3 * 256)) * 256
IN_SIZES = (GROUP_WIDTH, GROUP_WIDTH,
            GROUP_WIDTH, GROUP_WIDTH, GROUP_WIDTH,
            CONV_CH, CONV_CH,
            GROUP_WIDTH, KV_WIDTH, KV_WIDTH)
IN_WIDTH = sum(IN_SIZES)
RMS_EPS = 1e-6
LN_EPS = 1e-5

kernel_name = "hymba_style_hybrid_encoder_block"


def rms_norm(x, g):
    xf = x.astype(jnp.float32)
    y = xf * lax.rsqrt(jnp.mean(xf * xf, axis=-1, keepdims=True) + RMS_EPS)
    return (y * g.astype(jnp.float32)).astype(x.dtype)


def layer_norm_stats(x):
    xf = x.astype(jnp.float32)
    mu = jnp.mean(xf, axis=-1, keepdims=True)
    xc = xf - mu
    return xc * lax.rsqrt(jnp.mean(xc * xc, axis=-1, keepdims=True) + LN_EPS)


def split_heads(t):
    return t.reshape(t.shape[0], t.shape[1], -1, HEAD_DIM)


def t5_buckets(rel):
    nb = REL_BUCKETS // 2
    max_exact = nb // 2
    ret = jnp.where(rel > 0, nb, 0)
    n = jnp.abs(rel)
    nf = jnp.maximum(n, 1).astype(jnp.float32)
    large = max_exact + (jnp.log(nf / max_exact) / math.log(REL_MAX_DIST / max_exact)
                         * (nb - max_exact)).astype(jnp.int32)
    large = jnp.minimum(large, nb - 1)
    return ret + jnp.where(n < max_exact, n, large)


def sgu_branch(u, v, w_s, b_s):
    bn, s, w = u.shape
    nc = s // SGU_CHUNK
    u = jax.nn.gelu(u)
    v = jax.nn.gelu(v).reshape(bn, nc, SGU_CHUNK, SGU_GROUPS, HEAD_DIM)
    vn = layer_norm_stats(v).astype(u.dtype)
    mixed = jnp.einsum('gpq,bcqgd->bcpgd', w_s, vn) + b_s.T[None, None, :, :, None]
    return u * mixed.reshape(bn, s, w)


def dilated_pattern(q, k, v, rel_table, window, dil):
    bn, s, h, dh = q.shape
    half = window // (2 * dil)
    blk = half
    L = s // dil
    nb = -(-L // blk)
    lp = nb * blk

    def to_sub(t):
        return t.reshape(bn, L, dil, h, dh).transpose(0, 2, 1, 3, 4)

    qs = jnp.pad(to_sub(q), ((0, 0), (0, 0), (0, lp - L), (0, 0), (0, 0)))
    qs = qs.reshape(bn, dil, nb, blk, h, dh)
    pad_kv = ((0, 0), (0, 0), (blk, lp - L + blk), (0, 0), (0, 0))

    def band(t):
        t = jnp.pad(to_sub(t), pad_kv).reshape(bn, dil, nb + 2, blk, h, dh)
        return jnp.concatenate([t[:, :, :-2], t[:, :, 1:-1], t[:, :, 2:]], axis=3)

    kb, vb = band(k), band(v)
    sc = jnp.einsum('brnqhd,brnkhd->brnhqk', qs, kb, preferred_element_type=jnp.float32)
    off = jnp.arange(3 * blk)[None, :] - blk - jnp.arange(blk)[:, None]
    key_idx = jnp.arange(nb)[:, None] * blk - blk + jnp.arange(3 * blk)[None, :]
    valid = (jnp.abs(off) <= half)[None] & ((key_idx >= 0) & (key_idx < L))[:, None, :]
    bias = rel_table[t5_buckets(off * dil)].astype(jnp.float32).transpose(2, 0, 1)
    sc = sc + bias[None, None, None]
    sc = jnp.where(valid[None, None, :, None], sc, -1e30)
    lse = jax.nn.logsumexp(sc, axis=-1)
    p = jnp.exp(sc - lse[..., None])
    o = jnp.einsum('brnhqk,brnkhd->brnqhd', p.astype(v.dtype), vb)
    o = o.reshape(bn, dil, lp, h, dh)[:, :, :L].transpose(0, 2, 1, 3, 4).reshape(bn, s, h, dh)
    lse = lse.transpose(0, 1, 2, 4, 3).reshape(bn, dil, lp, h)[:, :, :L]
    lse = lse.transpose(0, 2, 1, 3).reshape(bn, s, h)
    return o, lse


def dilated_mixture(q, k, v, rel_table):
    outs, lses = [], []
    for window, dil in DIL_PATTERNS:
        o, lse = dilated_pattern(q, k, v, rel_table, window, dil)
        outs.append(o)
        lses.append(lse)
    w = jax.nn.softmax(jnp.stack(lses, axis=0), axis=0)
    return jnp.einsum('gbsh,gbshd->bshd', w.astype(v.dtype), jnp.stack(outs, axis=0))


def conv_branch(a, gate, w_dw, b_dw, ln_g, ln_b):
    hdn = a * jax.nn.sigmoid(gate)
    pad = CONV_WIDTH // 2
    hdn = lax.conv_general_dilated(hdn, w_dw[:, None, :], (1,), [(pad, pad)],
                                   dimension_numbers=('NWC', 'WIO', 'NWC'),
                                   feature_group_count=hdn.shape[-1]) + b_dw
    hdn = (layer_norm_stats(hdn) * ln_g.astype(jnp.float32) + ln_b.astype(jnp.float32)).astype(a.dtype)
    return jax.nn.silu(hdn)


def rope_axis(x, pos):
    half = x.shape[-1] // 2
    freqs = ROPE_THETA ** (-jnp.arange(half, dtype=jnp.float32) / half)
    ang = pos.astype(jnp.float32)[:, None] * freqs[None, :]
    cos = jnp.cos(ang)[:, None, :]
    sin = jnp.sin(ang)[:, None, :]
    xf = x.astype(jnp.float32)
    x1, x2 = xf[..., :half], xf[..., half:]
    return jnp.concatenate([x1 * cos - x2 * sin, x2 * cos + x1 * sin], axis=-1).astype(x.dtype)


def axial_rope(x, row, col):
    d2 = x.shape[-1] // 2
    return jnp.concatenate([rope_axis(x[..., :d2], row), rope_axis(x[..., d2:], col)], axis=-1)


def gqa_branch(q, k, v):
    bn, s, hq, dh = q.shape
    hkv = k.shape[2]
    g = hq // hkv
    nq = s // Q_BLOCK
    qb = q.reshape(bn, nq, Q_BLOCK, hkv, g, dh).transpose(1, 0, 2, 3, 4, 5)

    def block(qblk):
        sc = jnp.einsum('bqhgd,bkhd->bhgqk', qblk, k, preferred_element_type=jnp.float32)
        p = jax.nn.softmax(sc, axis=-1)
        return jnp.einsum('bhgqk,bkhd->bqhgd', p.astype(v.dtype), v)

    o = lax.map(block, qb)
    return o.transpose(1, 0, 2, 3, 4, 5).reshape(bn, s, hq, dh)


def setup_inputs(seed: int = 0) -> dict:
    key = jax.random.key(seed)
    ks = jax.random.split(key, 20)
    f32 = jnp.float32
    nrm = lambda k, shape, scale: jax.random.normal(k, shape, f32) * scale
    gain = lambda k, shape: 1.0 + 0.02 * jax.random.normal(k, shape, f32)
    return {
        "x": jax.random.normal(ks[0], (BATCH, SEQ, D_MODEL), f32),
        "rel_bias": nrm(ks[1], (REL_BUCKETS, DIL_HEADS), 0.5),
        "norm1_g": gain(ks[2], (DEPTH, D_MODEL)),
        "w_in": nrm(ks[3], (DEPTH, D_MODEL, IN_WIDTH), D_MODEL ** -0.5),
        "sgu_w": nrm(ks[4], (DEPTH, SGU_GROUPS, SGU_CHUNK, SGU_CHUNK), SGU_CHUNK ** -0.5),
        "sgu_b": gain(ks[5], (DEPTH, SGU_GROUPS, SGU_CHUNK)),
        "dil_qn_g": gain(ks[6], (DEPTH, HEAD_DIM)),
        "dil_kn_g": gain(ks[7], (DEPTH, HEAD_DIM)),
        "conv_w": nrm(ks[8], (DEPTH, CONV_WIDTH, CONV_CH), CONV_WIDTH ** -0.5),
        "conv_b": nrm(ks[9], (DEPTH, CONV_CH), 0.02),
        "conv_ln_g": gain(ks[10], (DEPTH, CONV_CH)),
        "conv_ln_b": nrm(ks[11], (DEPTH, CONV_CH), 0.02),
        "gqa_qn_g": gain(ks[12], (DEPTH, HEAD_DIM)),
        "gqa_kn_g": gain(ks[13], (DEPTH, HEAD_DIM)),
        "mix_norm_g": gain(ks[14], (DEPTH, MIX_WIDTH)),
        "w_out": nrm(ks[15], (DEPTH, MIX_WIDTH, D_MODEL), MIX_WIDTH ** -0.5),
        "norm2_g": gain(ks[16], (DEPTH, D_MODEL)),
        "w_gate": nrm(ks[17], (DEPTH, D_MODEL, FFN_HIDDEN), D_MODEL ** -0.5),
        "w_up": nrm(ks[18], (DEPTH, D_MODEL, FFN_HIDDEN), D_MODEL ** -0.5),
        "w_down": nrm(ks[19], (DEPTH, FFN_HIDDEN, D_MODEL), FFN_HIDDEN ** -0.5),
    }


def reference(x, rel_bias, norm1_g, w_in, sgu_w, sgu_b, dil_qn_g, dil_kn_g, conv_w, conv_b,
              conv_ln_g, conv_ln_b, gqa_qn_g, gqa_kn_g, mix_norm_g, w_out, norm2_g,
              w_gate, w_up, w_down):
    bn, s, _ = x.shape
    rows = s // GRID_W
    row = jnp.repeat(jnp.arange(rows), GRID_W)
    col = jnp.tile(jnp.arange(GRID_W), rows)
    split_at = np.cumsum(IN_SIZES)[:-1].tolist()
    scale = HEAD_DIM ** -0.5
    for l in range(DEPTH):
        h = rms_norm(x, norm1_g[l])
        z = h @ w_in[l]
        a_u, a_v, b_q, b_k, b_v, c_a, c_g, d_q, d_k, d_v = jnp.split(z, split_at, axis=-1)
        y_a = sgu_branch(a_u, a_v, sgu_w[l], sgu_b[l])
        qb = rms_norm(split_heads(b_q), dil_qn_g[l]) * scale
        kb = rms_norm(split_heads(b_k), dil_kn_g[l])
        y_b = dilated_mixture(qb, kb, split_heads(b_v), rel_bias).reshape(bn, s, GROUP_WIDTH)
        y_c = conv_branch(c_a, c_g, conv_w[l], conv_b[l], conv_ln_g[l], conv_ln_b[l])
        qd = axial_rope(rms_norm(split_heads(d_q), gqa_qn_g[l]), row, col) * scale
        kd = axial_rope(rms_norm(split_heads(d_k), gqa_kn_g[l]), row, col)
        y_d = gqa_branch(qd, kd, split_heads(d_v)).reshape(bn, s, GROUP_WIDTH)
        y = jnp.stack([y_a, y_b, y_c, y_d], axis=2)
        y = rms_norm(y, mix_norm_g[l].reshape(N_MIXERS, GROUP_WIDTH)).reshape(bn, s, MIX_WIDTH)
        x = x + y @ w_out[l]
        h = rms_norm(x, norm2_g[l])
        x = x + (jax.nn.silu(h @ w_gate[l]) * (h @ w_up[l])) @ w_down[l]
    return x
```

```python
import functools
import math

import numpy as np
import jax
import jax.numpy as jnp
from jax import lax
from jax.experimental import pallas as pl
from jax.experimental.pallas import tpu as pltpu

F32 = jnp.float32
BF16 = jnp.bfloat16

HEAD_DIM = 64
GROUP_WIDTH = 512
N_HEADS = GROUP_WIDTH // HEAD_DIM
SGU_CHUNK = 128
DIL_PATTERNS = ((128, 1), (512, 4), (2048, 16))
CONV_WIDTH = 31
GQA_KV_HEADS = 2
KV_WIDTH = GQA_KV_HEADS * HEAD_DIM
GRID_W = 64
ROPE_THETA = 10000.0
REL_BUCKETS = 32
REL_MAX_DIST = 1024
RMS_EPS = 1e-6
LN_EPS = 1e-5
MASK_VALUE = -1e30

LANES = 128
SUBLANES = 8
VMEM_LIMIT = 56 * 1024 * 1024

Q_BLOCK = 128
CONV_ROWS = 64
CONV_HALO = 16


def _params(*semantics):
    return pltpu.CompilerParams(dimension_semantics=semantics, vmem_limit_bytes=VMEM_LIMIT)


def _dot(a, b):
    return jnp.dot(a, b, preferred_element_type=F32)


def _dot_hilo(a, m):
    hi = a.astype(BF16)
    lo = (a - hi.astype(F32)).astype(BF16)
    return _dot(hi, m) + _dot(lo, m)


def _head_indicator(width):
    ind = np.zeros((width, LANES), np.float32)
    for h in range(width // HEAD_DIM):
        ind[h * HEAD_DIM:(h + 1) * HEAD_DIM, h] = 1.0
    return jnp.asarray(ind, BF16), jnp.asarray(ind.T, BF16)


def _head_rms(x, gain, ind, ind_t):
    ms = _dot_hilo(x * x, ind) * (1.0 / HEAD_DIM)
    r = _dot_hilo(lax.rsqrt(ms + RMS_EPS), ind_t)
    return x * r * gain


def _in_proj_kernel(x_ref, g_ref, w_ref, wkv_ref, z_ref, zkv_ref, h_ref):
    @pl.when(pl.program_id(1) == 0)
    def _():
        x = x_ref[...]
        r = lax.rsqrt(jnp.mean(x * x, axis=-1, keepdims=True) + RMS_EPS)
        h = (x * r * g_ref[...]).astype(BF16)
        h_ref[...] = h
        zkv_ref[...] = _dot(h, wkv_ref[...]).astype(BF16)

    z_ref[...] = _dot(h_ref[...], w_ref[...]).astype(BF16)


def _in_proj(x, g, w_main, w_kv, tm=1024, tn=1024):
    m, d = x.shape
    n = w_main.shape[1]
    nkv = w_kv.shape[1]
    return pl.pallas_call(
        _in_proj_kernel,
        grid=(m // tm, n // tn),
        in_specs=[
            pl.BlockSpec((tm, d), lambda i, j: (i, 0)),
            pl.BlockSpec((1, d), lambda i, j: (0, 0)),
            pl.BlockSpec((d, tn), lambda i, j: (0, j)),
            pl.BlockSpec((d, nkv), lambda i, j: (0, 0)),
        ],
        out_specs=[
            pl.BlockSpec((tm, tn), lambda i, j: (i, j)),
            pl.BlockSpec((tm, nkv), lambda i, j: (i, 0)),
        ],
        out_shape=[jax.ShapeDtypeStruct((m, n), BF16), jax.ShapeDtypeStruct((m, nkv), BF16)],
        scratch_shapes=[pltpu.VMEM((tm, d), BF16)],
        compiler_params=_params("parallel", "arbitrary"),
        name="in_proj",
    )(x, g, w_main, w_kv)


def _sgu_kernel(u_ref, v_ref, w_ref, b_ref, ind_ref, indt_ref, o_ref):
    ind = ind_ref[...]
    ind_t = indt_ref[...]
    u = jax.nn.gelu(u_ref[...].astype(F32))
    v = jax.nn.gelu(v_ref[...].astype(F32))
    mu = _dot_hilo(_dot_hilo(v, ind) * (1.0 / HEAD_DIM), ind_t)
    xc = v - mu
    var = _dot_hilo(xc * xc, ind) * (1.0 / HEAD_DIM)
    vn = (xc * _dot_hilo(lax.rsqrt(var + LN_EPS), ind_t)).astype(BF16)
    bias = b_ref[...]
    for c in range(u.shape[0] // SGU_CHUNK):
        rows = slice(c * SGU_CHUNK, (c + 1) * SGU_CHUNK)
        vc = vn[rows]
        mixed = jnp.concatenate(
            [_dot(w_ref[g], vc[:, g * HEAD_DIM:(g + 1) * HEAD_DIM]) for g in range(N_HEADS)], axis=1)
        o_ref[rows, :] = (u[rows] * (mixed + bias)).astype(BF16)


def _sgu(z, w_s, b_lanes, ind, ind_t, tt=1024):
    m = z.shape[0]
    gw = GROUP_WIDTH
    return pl.pallas_call(
        _sgu_kernel,
        grid=(m // tt,),
        in_specs=[
            pl.BlockSpec((tt, gw), lambda i: (i, 0)),
            pl.BlockSpec((tt, gw), lambda i: (i, 1)),
            pl.BlockSpec(w_s.shape, lambda i: (0, 0, 0)),
            pl.BlockSpec(b_lanes.shape, lambda i: (0, 0)),
            pl.BlockSpec(ind.shape, lambda i: (0, 0)),
            pl.BlockSpec(ind_t.shape, lambda i: (0, 0)),
        ],
        out_specs=pl.BlockSpec((tt, gw), lambda i: (i, 0)),
        out_shape=jax.ShapeDtypeStruct((m, gw), BF16),
        compiler_params=_params("parallel"),
        name="sgu",
    )(z, z, w_s, b_lanes, ind, ind_t)


def _dilated_kernel(q_ref, k_ref, v_ref, qg_ref, kg_ref, tab_ref, ind_ref, indt_ref, o_ref,
                    qs_ref, ks_ref, bias_ref, *, seq):
    ind = ind_ref[...]
    ind_t = indt_ref[...]
    scale = HEAD_DIM ** -0.5
    qs_ref[...] = (_head_rms(q_ref[...].astype(F32), qg_ref[...], ind, ind_t) * scale).astype(BF16)
    ks_ref[...] = _head_rms(k_ref[...].astype(F32), kg_ref[...], ind, ind_t).astype(BF16)
    heads = q_ref.shape[1] // HEAD_DIM
    width = tab_ref.shape[-1]
    for hh in range(heads):
        row = jnp.broadcast_to(tab_ref[hh:hh + 1, :], (Q_BLOCK, width))
        bias_ref[hh] = pltpu.roll(row, 0, 1, stride=1, stride_axis=0)
    for hh in range(heads):
        cols = slice(hh * HEAD_DIM, (hh + 1) * HEAD_DIM)
        k_h = ks_ref[:, cols]
        v_h = v_ref[:, cols]
        for blk in range(seq // Q_BLOCK):
            t0 = blk * Q_BLOCK
            start = seq - Q_BLOCK - t0
            s = lax.dot_general(qs_ref[t0:t0 + Q_BLOCK, cols], k_h, (((1,), (1,)), ((), ())),
                                preferred_element_type=F32)
            s = s + bias_ref[hh, :, start:start + seq]
            m = jnp.max(s, axis=-1, keepdims=True)
            p = jnp.exp(s - m)
            l = jnp.sum(p, axis=-1, keepdims=True)
            o = _dot(p.astype(BF16), v_h)
            o_ref[t0:t0 + Q_BLOCK, cols] = (o / l).astype(BF16)


def _dilated(z, q_gain, k_gain, table, ind, ind_t, seq, heads_per_step=2):
    m = z.shape[0]
    w = heads_per_step * HEAD_DIM
    per = GROUP_WIDTH // w
    tab = table.reshape(per, heads_per_step, table.shape[-1])
    return pl.pallas_call(
        functools.partial(_dilated_kernel, seq=seq),
        grid=(m // seq, per),
        in_specs=[
            pl.BlockSpec((seq, w), lambda b, j: (b, 2 * per + j)),
            pl.BlockSpec((seq, w), lambda b, j: (b, 3 * per + j)),
            pl.BlockSpec((seq, w), lambda b, j: (b, 4 * per + j)),
            pl.BlockSpec((1, w), lambda b, j: (0, 0)),
            pl.BlockSpec((1, w), lambda b, j: (0, 0)),
            pl.BlockSpec((None, heads_per_step, tab.shape[-1]), lambda b, j: (j, 0, 0)),
            pl.BlockSpec(ind.shape, lambda b, j: (0, 0)),
            pl.BlockSpec(ind_t.shape, lambda b, j: (0, 0)),
        ],
        out_specs=pl.BlockSpec((seq, w), lambda b, j: (b, j)),
        out_shape=jax.ShapeDtypeStruct((m, GROUP_WIDTH), BF16),
        scratch_shapes=[
            pltpu.VMEM((seq, w), BF16),
            pltpu.VMEM((seq, w), BF16),
            pltpu.VMEM((heads_per_step, Q_BLOCK, tab.shape[-1]), F32),
        ],
        compiler_params=_params("parallel", "parallel"),
        name="dilated_attention",
    )(z, z, z, q_gain, k_gain, tab, ind, ind_t)


def _t5_buckets(rel):
    nb = REL_BUCKETS // 2
    max_exact = nb // 2
    ret = jnp.where(rel > 0, nb, 0)
    n = jnp.abs(rel)
    nf = jnp.maximum(n, 1).astype(F32)
    large = max_exact + (jnp.log(nf / max_exact) / math.log(REL_MAX_DIST / max_exact)
                         * (nb - max_exact)).astype(jnp.int32)
    large = jnp.minimum(large, nb - 1)
    return ret + jnp.where(n < max_exact, n, large)


def _distance_table(rel_bias, seq):
    d = np.arange(2 * seq) - (seq - Q_BLOCK)
    count = np.zeros(2 * seq, np.int64)
    for window, dil in DIL_PATTERNS:
        count += ((d % dil == 0) & (np.abs(d) <= (window // (2 * dil)) * dil)).astype(np.int64)
    log_count = np.where(count > 0, np.log(np.maximum(count, 1)), MASK_VALUE).astype(np.float32)
    bias = rel_bias[_t5_buckets(jnp.asarray(d, jnp.int32))].astype(F32)
    return bias.T + jnp.asarray(log_count)[None, :]


def _conv_kernel(a_ref, g_ref, w_ref, b_ref, lng_ref, lnb_ref, o_ref, pad_ref, *, seq):
    width = a_ref.shape[1]
    pad_ref[0:CONV_HALO, :] = jnp.zeros((CONV_HALO, width), F32)
    pad_ref[CONV_HALO + seq:, :] = jnp.zeros((CONV_HALO, width), F32)
    pad_ref[CONV_HALO:CONV_HALO + seq, :] = a_ref[...].astype(F32) * jax.nn.sigmoid(g_ref[...].astype(F32))
    first = CONV_HALO - CONV_WIDTH // 2

    def body(c, carry):
        r0 = pl.multiple_of(c * CONV_ROWS, CONV_ROWS)
        slab_rows = CONV_ROWS + 2 * CONV_HALO
        slab = pad_ref[pl.ds(r0, slab_rows), :]
        acc = jnp.zeros((CONV_ROWS, width), F32)
        for res in range(SUBLANES):
            rolled = slab if res == 0 else pltpu.roll(slab, slab_rows - res, 0)
            for j in range(CONV_WIDTH):
                if (first + j) % SUBLANES == res:
                    base = first + j - res
                    acc = acc + w_ref[j:j + 1, :] * rolled[base:base + CONV_ROWS]
        acc = acc + b_ref[...]
        mu = jnp.mean(acc, axis=-1, keepdims=True)
        xc = acc - mu
        y = xc * lax.rsqrt(jnp.mean(xc * xc, axis=-1, keepdims=True) + LN_EPS)
        y = y * lng_ref[...] + lnb_ref[...]
        o_ref[pl.ds(r0, CONV_ROWS), :] = jax.nn.silu(y).astype(BF16)
        return carry

    lax.fori_loop(0, seq // CONV_ROWS, body, 0)


def _conv(z, w, b, ln_g, ln_b, seq):
    m = z.shape[0]
    gw = GROUP_WIDTH
    vec = pl.BlockSpec((1, gw), lambda i: (0, 0))
    return pl.pallas_call(
        functools.partial(_conv_kernel, seq=seq),
        grid=(m // seq,),
        in_specs=[
            pl.BlockSpec((seq, gw), lambda i: (i, 5)),
            pl.BlockSpec((seq, gw), lambda i: (i, 6)),
            pl.BlockSpec(w.shape, lambda i: (0, 0)),
            vec, vec, vec,
        ],
        out_specs=pl.BlockSpec((seq, gw), lambda i: (i, 0)),
        out_shape=jax.ShapeDtypeStruct((m, gw), BF16),
        scratch_shapes=[pltpu.VMEM((seq + 2 * CONV_HALO, gw), F32)],
        compiler_params=_params("parallel"),
        name="conv_module",
    )(z, z, w, b, ln_g, ln_b)


def _rope(x, cos, sin_signed):
    half = HEAD_DIM // 4
    n = x.shape[1]
    lane = lax.broadcasted_iota(jnp.int32, x.shape, 1)
    partner = jnp.where(lane % (2 * half) < half, pltpu.roll(x, n - half, 1), pltpu.roll(x, half, 1))
    return x * cos + partner * sin_signed


def _gqa_kernel(q_ref, kv_ref, qg_ref, kg_ref, cos_ref, sin_ref, ind_ref, indt_ref, o_ref,
                qs_ref, ks_ref, acc_ref, *, seq):
    ind = ind_ref[...]
    ind_t = indt_ref[...]
    scale = HEAD_DIM ** -0.5
    q = _head_rms(q_ref[...].astype(F32), qg_ref[...], ind, ind_t)
    qs_ref[...] = (_rope(q, cos_ref[...], sin_ref[...]) * scale).astype(BF16)
    k = _head_rms(kv_ref[:, :KV_WIDTH].astype(F32), kg_ref[...], ind[:KV_WIDTH], ind_t[:, :KV_WIDTH])
    ks_ref[...] = _rope(k, cos_ref[:, :KV_WIDTH], sin_ref[:, :KV_WIDTH]).astype(BF16)
    group = N_HEADS // GQA_KV_HEADS
    for kvh in range(GQA_KV_HEADS):
        k_h = ks_ref[:, kvh * HEAD_DIM:(kvh + 1) * HEAD_DIM]
        v_h = kv_ref[:, KV_WIDTH + kvh * HEAD_DIM:KV_WIDTH + (kvh + 1) * HEAD_DIM]

        def body(blk, carry):
            r0 = pl.multiple_of(blk * Q_BLOCK, Q_BLOCK)
            rows = pl.ds(r0, Q_BLOCK)
            q4 = jnp.concatenate(
                [qs_ref[rows, (kvh * group + i) * HEAD_DIM:(kvh * group + i + 1) * HEAD_DIM]
                 for i in range(group)], axis=0)
            s = lax.dot_general(q4, k_h, (((1,), (1,)), ((), ())), preferred_element_type=F32)
            m = jnp.max(s, axis=-1, keepdims=True)
            p = jnp.exp(s - m)
            l = jnp.sum(p, axis=-1, keepdims=True)
            o = _dot(p.astype(BF16), v_h) / l
            for i in range(group):
                h = kvh * group + i
                acc_ref[rows, h * HEAD_DIM:(h + 1) * HEAD_DIM] = o[i * Q_BLOCK:(i + 1) * Q_BLOCK]
            return carry

        lax.fori_loop(0, seq // Q_BLOCK, body, 0)
    o_ref[...] = acc_ref[...].astype(BF16)


def _gqa(z, zkv, q_gain, k_gain, cos, sin_signed, ind, ind_t, seq):
    m = z.shape[0]
    gw = GROUP_WIDTH
    full = lambda a: pl.BlockSpec(a.shape, lambda b: (0,) * a.ndim)
    return pl.pallas_call(
        functools.partial(_gqa_kernel, seq=seq),
        grid=(m // seq,),
        in_specs=[
            pl.BlockSpec((seq, gw), lambda b: (b, 7)),
            pl.BlockSpec((seq, 2 * KV_WIDTH), lambda b: (b, 0)),
            full(q_gain), full(k_gain), full(cos), full(sin_signed), full(ind), full(ind_t),
        ],
        out_specs=pl.BlockSpec((seq, gw), lambda b: (b, 0)),
        out_shape=jax.ShapeDtypeStruct((m, gw), BF16),
        scratch_shapes=[
            pltpu.VMEM((seq, gw), BF16),
            pltpu.VMEM((seq, KV_WIDTH), BF16),
            pltpu.VMEM((seq, gw), F32),
        ],
        compiler_params=_params("parallel"),
        name="gqa_attention",
    )(z, zkv, q_gain, k_gain, cos, sin_signed, ind, ind_t)


def _rope_tables(seq):
    half = HEAD_DIM // 4
    t = jnp.arange(seq)
    freqs = ROPE_THETA ** (-jnp.arange(half, dtype=F32) / half)

    def axis_tables(pos):
        ang = pos.astype(F32)[:, None] * freqs[None, :]
        c, s = jnp.cos(ang), jnp.sin(ang)
        return jnp.concatenate([c, c], axis=-1), jnp.concatenate([-s, s], axis=-1)

    c_row, s_row = axis_tables(t // GRID_W)
    c_col, s_col = axis_tables(t % GRID_W)
    cos = jnp.concatenate([c_row, c_col], axis=-1)
    sin = jnp.concatenate([s_row, s_col], axis=-1)
    return jnp.tile(cos, (1, N_HEADS)), jnp.tile(sin, (1, N_HEADS))


def _out_proj_kernel(ya_ref, yb_ref, yc_ref, yd_ref, mg_ref, w_ref, x_ref, g2_ref, xo_ref, h_ref):
    parts = []
    for i, y_ref in enumerate((ya_ref, yb_ref, yc_ref, yd_ref)):
        y = y_ref[...].astype(F32)
        r = lax.rsqrt(jnp.mean(y * y, axis=-1, keepdims=True) + RMS_EPS)
        parts.append((y * r * mg_ref[i:i + 1, :]).astype(BF16))
    x = x_ref[...] + _dot(jnp.concatenate(parts, axis=1), w_ref[...])
    xo_ref[...] = x
    r = lax.rsqrt(jnp.mean(x * x, axis=-1, keepdims=True) + RMS_EPS)
    h_ref[...] = (x * r * g2_ref[...]).astype(BF16)


def _out_proj(ys, mix_gain, w_out, x, g2, tm=512):
    m, d = x.shape
    gw = GROUP_WIDTH
    yspec = pl.BlockSpec((tm, gw), lambda i: (i, 0))
    return pl.pallas_call(
        _out_proj_kernel,
        grid=(m // tm,),
        in_specs=[
            yspec, yspec, yspec, yspec,
            pl.BlockSpec(mix_gain.shape, lambda i: (0, 0)),
            pl.BlockSpec(w_out.shape, lambda i: (0, 0)),
            pl.BlockSpec((tm, d), lambda i: (i, 0)),
            pl.BlockSpec((1, d), lambda i: (0, 0)),
        ],
        out_specs=[pl.BlockSpec((tm, d), lambda i: (i, 0)), pl.BlockSpec((tm, d), lambda i: (i, 0))],
        out_shape=[jax.ShapeDtypeStruct((m, d), F32), jax.ShapeDtypeStruct((m, d), BF16)],
        compiler_params=_params("parallel"),
        name="out_proj",
    )(*ys, mix_gain, w_out, x, g2)


def _ffn_kernel(h_ref, x_ref, wg_ref, wu_ref, wd_ref, o_ref):
    @pl.when(pl.program_id(1) == 0)
    def _():
        o_ref[...] = x_ref[...]

    h = h_ref[...]
    a = jax.nn.silu(_dot(h, wg_ref[...])) * _dot(h, wu_ref[...])
    o_ref[...] += _dot(a.astype(BF16), wd_ref[...])


def _ffn(h, x, w_gate, w_up, w_down, tm=512, th=512):
    m, d = x.shape
    hidden = w_gate.shape[1]
    return pl.pallas_call(
        _ffn_kernel,
        grid=(m // tm, hidden // th),
        in_specs=[
            pl.BlockSpec((tm, d), lambda i, j: (i, 0)),
            pl.BlockSpec((tm, d), lambda i, j: (i, 0)),
            pl.BlockSpec((d, th), lambda i, j: (0, j)),
            pl.BlockSpec((d, th), lambda i, j: (0, j)),
            pl.BlockSpec((th, d), lambda i, j: (j, 0)),
        ],
        out_specs=pl.BlockSpec((tm, d), lambda i, j: (i, 0)),
        out_shape=jax.ShapeDtypeStruct((m, d), F32),
        compiler_params=_params("parallel", "arbitrary"),
        name="swiglu_ffn",
    )(h, x, w_gate, w_up, w_down)


def kernel(x, rel_bias, norm1_g, w_in, sgu_w, sgu_b, dil_qn_g, dil_kn_g, conv_w, conv_b, conv_ln_g,
           conv_ln_b, gqa_qn_g, gqa_kn_g, mix_norm_g, w_out, norm2_g, w_gate, w_up, w_down):
    bn, seq, d = x.shape
    depth = w_in.shape[0]
    gw = GROUP_WIDTH
    n_main = 8 * gw
    ind, ind_t = _head_indicator(gw)
    ind2, ind2_t = _head_indicator(2 * HEAD_DIM)
    cos, sin_signed = _rope_tables(seq)
    table = _distance_table(rel_bias, seq)
    tile_heads = lambda g, n: jnp.tile(g.astype(F32), n)[None, :]

    xf = x.reshape(bn * seq, d)
    for l in range(depth):
        w_l = w_in[l].astype(BF16)
        z, zkv = _in_proj(xf, norm1_g[l][None, :], w_l[:, :n_main], w_l[:, n_main:])
        y_a = _sgu(z, sgu_w[l].astype(BF16), jnp.repeat(sgu_b[l].T, HEAD_DIM, axis=1), ind, ind_t)
        y_b = _dilated(z, tile_heads(dil_qn_g[l], 2), tile_heads(dil_kn_g[l], 2), table, ind2, ind2_t, seq)
        y_c = _conv(z, conv_w[l], conv_b[l][None, :], conv_ln_g[l][None, :], conv_ln_b[l][None, :], seq)
        y_d = _gqa(z, zkv, tile_heads(gqa_qn_g[l], N_HEADS), tile_heads(gqa_kn_g[l], GQA_KV_HEADS),
                   cos, sin_signed, ind, ind_t, seq)
        xf, h2 = _out_proj((y_a, y_b, y_c, y_d), mix_norm_g[l].reshape(4, gw), w_out[l].astype(BF16),
                           xf, norm2_g[l][None, :])
        xf = _ffn(h2, xf, w_gate[l].astype(BF16), w_up[l].astype(BF16), w_down[l].astype(BF16))
    return xf.reshape(bn, seq, d)
```
